```python
import math
import jax, jax.numpy as jnp
from jax import lax
import numpy as np

D_MODEL = 1024
BATCH = 8
SEQ = 2048
DEPTH = 1
DEC_BATCH = 128
DEC_SEQ = 8
PAST_LEN = 16384
PAGE_SIZE = 128

D_CONF = D_MODEL
CONF_KERNEL = 31
D_SSM = D_MODEL
SSM_HEAD_DIM = 64
SSM_HEADS = D_SSM // SSM_HEAD_DIM
SSM_GROUPS = 2
SSM_HPG = SSM_HEADS // SSM_GROUPS
SSM_STATE = 128
SSM_CONV = 4
SSD_CHUNK = 128
XBC_DIM = D_SSM + 2 * SSM_GROUPS * SSM_STATE
D_MIX = D_CONF + D_SSM
IN_COLS = 2 * D_CONF + D_SSM + XBC_DIM + SSM_HEADS
MOE_GROUPS = 4
EXPERTS_PER_GROUP = 8
N_EXPERTS = MOE_GROUPS * EXPERTS_PER_GROUP
TOP_K_INNER = 2
D_EXPERT = D_MODEL // 2
EPS = 1e-6

kernel_name = 'hymba_conformer_ssd_hmoe_step'


def rmsnorm(x, g):
    xf = x.astype(jnp.float32)
    y = xf * lax.rsqrt(jnp.mean(xf * xf, axis=-1, keepdims=True) + EPS)
    return (y * g.astype(jnp.float32)).astype(x.dtype)


def layernorm(x, g, b):
    xf = x.astype(jnp.float32)
    mu = jnp.mean(xf, axis=-1, keepdims=True)
    xc = xf - mu
    y = xc * lax.rsqrt(jnp.mean(xc * xc, axis=-1, keepdims=True) + EPS)
    return (y * g.astype(jnp.float32) + b.astype(jnp.float32)).astype(x.dtype)


def causal_dwconv(buf, u, w, b):
    full = jnp.concatenate([buf.astype(u.dtype), u], axis=1)
    out = lax.conv_general_dilated(full, w[:, None, :].astype(u.dtype), window_strides=(1,),
                                   padding='VALID', dimension_numbers=('NWC', 'WIO', 'NWC'),
                                   feature_group_count=u.shape[-1])
    new_buf = full[:, full.shape[1] - (w.shape[0] - 1):]
    return out + b.astype(u.dtype), new_buf


def ssd_scan(x, dt, a, b_in, c_in, h0, chunk):
    bt, L = x.shape[0], x.shape[1]
    nc = L // chunk
    f32 = jnp.float32
    def chunked(t):
        return t.astype(f32).reshape((bt, nc, chunk) + t.shape[2:])
    xc, dtc, bc, cc = chunked(x), chunked(dt), chunked(b_in), chunked(c_in)
    acum = jnp.cumsum(dtc * a.astype(f32), axis=2)
    xdt = xc * dtc[..., None]
    causal = jnp.tril(jnp.ones((chunk, chunk), dtype=bool))[None, None, :, :, None, None]
    seg = acum[:, :, :, None] - acum[:, :, None, :]
    decay = jnp.exp(jnp.where(causal, seg, -jnp.inf))
    cb = jnp.einsum('bcign,bcjgn->bcijg', cc, bc)
    y_diag = jnp.einsum('bcijg,bcijgr,bcjgrp->bcigrp', cb, decay, xdt)
    decay_to_end = jnp.exp(acum[:, :, -1:] - acum)
    states = jnp.einsum('bcjgn,bcjgr,bcjgrp->bcgrpn', bc, decay_to_end, xdt)
    chunk_decay = jnp.exp(acum[:, :, -1])
    def step(h, inp):
        s, d = inp
        return d[..., None, None] * h + s, h
    h_last, h_prev = lax.scan(step, h0.astype(f32),
                              (jnp.moveaxis(states, 1, 0), jnp.moveaxis(chunk_decay, 1, 0)))
    h_prev = jnp.moveaxis(h_prev, 0, 1)
    y_off = jnp.einsum('bcign,bcigr,bcgrpn->bcigrp', cc, jnp.exp(acum), h_prev)
    y = (y_diag + y_off).reshape(x.shape)
    return y.astype(x.dtype), h_last.astype(h0.dtype)


def mixer(h, conf_buf, ssm_buf, ssm_h, w_in, conf_dw_w, conf_dw_b, conf_ln_g, conf_ln_b,
          ssm_conv_w, ssm_conv_b, dt_bias, a_log, d_skip, ssm_norm_g, w_out):
    bt, L, _ = h.shape
    proj = jnp.einsum('bld,de->ble', h, w_in.astype(h.dtype))
    i1, i2, i3, i4 = D_CONF, 2 * D_CONF, 2 * D_CONF + D_SSM, 2 * D_CONF + D_SSM + XBC_DIM
    u_a, u_b, z, xbc, dt_raw = proj[..., :i1], proj[..., i1:i2], proj[..., i2:i3], proj[..., i3:i4], proj[..., i4:]
    u = u_a * jax.nn.sigmoid(u_b)
    v, conf_buf_new = causal_dwconv(conf_buf, u, conf_dw_w, conf_dw_b)
    v = jax.nn.silu(layernorm(v, conf_ln_g, conf_ln_b))
    xbc_c, ssm_buf_new = causal_dwconv(ssm_buf, xbc, ssm_conv_w, ssm_conv_b)
    xbc_c = jax.nn.silu(xbc_c)
    xs = xbc_c[..., :D_SSM].reshape(bt, L, SSM_GROUPS, SSM_HPG, SSM_HEAD_DIM)
    bs = xbc_c[..., D_SSM:D_SSM + SSM_GROUPS * SSM_STATE].reshape(bt, L, SSM_GROUPS, SSM_STATE)
    cs = xbc_c[..., D_SSM + SSM_GROUPS * SSM_STATE:].reshape(bt, L, SSM_GROUPS, SSM_STATE)
    dt = jax.nn.softplus(dt_raw.astype(jnp.float32) + dt_bias.astype(jnp.float32))
    dt = dt.reshape(bt, L, SSM_GROUPS, SSM_HPG)
    a = -jnp.exp(a_log.astype(jnp.float32)).reshape(SSM_GROUPS, SSM_HPG)
    h0 = ssm_h.reshape(bt, SSM_GROUPS, SSM_HPG, SSM_HEAD_DIM, SSM_STATE)
    chunk = math.gcd(L, SSD_CHUNK)
    y, h_new = ssd_scan(xs, dt, a, bs, cs, h0, chunk)
    y = y + d_skip.astype(y.dtype).reshape(SSM_GROUPS, SSM_HPG, 1) * xs
    yz = (y.reshape(bt, L, D_SSM) * jax.nn.silu(z)).reshape(bt, L, SSM_GROUPS, D_SSM // SSM_GROUPS)
    yz = rmsnorm(yz, jnp.ones((D_SSM // SSM_GROUPS,), jnp.float32)).reshape(bt, L, D_SSM)
    y_ssm = yz * ssm_norm_g.astype(yz.dtype)
    out = jnp.einsum('ble,ed->bld', jnp.concatenate([v, y_ssm], axis=-1), w_out.astype(h.dtype))
    h_new = h_new.reshape(bt, SSM_HEADS, SSM_HEAD_DIM, SSM_STATE)
    return out, conf_buf_new, ssm_buf_new, h_new


def hier_moe(h, rg_w, rg_b, re_w, re_b, w_gate, w_up, w_down):
    shp = h.shape
    t = h.reshape(-1, shp[-1])
    tf = t.astype(jnp.float32)
    lg = tf @ rg_w.astype(jnp.float32) + rg_b.astype(jnp.float32)
    pg = jax.nn.softmax(lg, axis=-1)
    grp = jnp.argmax(lg, axis=-1)
    g1 = jnp.take_along_axis(pg, grp[:, None], axis=-1)
    le_all = jnp.einsum('td,gde->tge', tf, re_w.astype(jnp.float32)) + re_b.astype(jnp.float32)
    le = jnp.take_along_axis(le_all, grp[:, None, None], axis=1)[:, 0]
    top_v, top_i = lax.top_k(le, TOP_K_INNER)
    g2 = jax.nn.softmax(top_v, axis=-1)
    eid = grp[:, None] * EXPERTS_PER_GROUP + top_i
    combine = jnp.sum(jax.nn.one_hot(eid, N_EXPERTS, dtype=jnp.float32) * (g1 * g2)[..., None], axis=1)
    combine = combine.astype(t.dtype)
    acc = jnp.zeros_like(t)
    for e in range(N_EXPERTS):
        hid = jax.nn.silu(t @ w_gate[e].astype(t.dtype)) * (t @ w_up[e].astype(t.dtype))
        acc = acc + combine[:, e:e + 1] * (hid @ w_down[e].astype(t.dtype))
    return acc.reshape(shp)


def setup_inputs(seed: int = 0) -> dict:
    key = jax.random.key(seed)
    ks = jax.random.split(key, 32)
    n = jax.random.normal
    f = jnp.float32
    dt0 = jnp.exp(jax.random.uniform(ks[10], (DEPTH, SSM_HEADS), f, math.log(1e-3), math.log(1e-1)))
    return {
        'x_prompt': n(ks[0], (BATCH, SEQ, D_MODEL), f),
        'x_sample': n(ks[1], (DEC_BATCH, DEC_SEQ, D_MODEL), f),
        'state_conf_conv': n(ks[2], (DEPTH, DEC_BATCH, CONF_KERNEL - 1, D_CONF), f),
        'state_ssm_conv': n(ks[3], (DEPTH, DEC_BATCH, SSM_CONV - 1, XBC_DIM), f),
        'state_ssm': 0.5 * n(ks[4], (DEPTH, DEC_BATCH, SSM_HEADS, SSM_HEAD_DIM, SSM_STATE), f),
        'ln_mix': 1.0 + 0.02 * n(ks[5], (DEPTH, D_MODEL), f),
        'w_in': n(ks[6], (DEPTH, D_MODEL, IN_COLS), f) * D_MODEL ** -0.5,
        'conf_dw_w': n(ks[7], (DEPTH, CONF_KERNEL, D_CONF), f) * CONF_KERNEL ** -0.5,
        'conf_dw_b': 0.02 * n(ks[8], (DEPTH, D_CONF), f),
        'conf_ln_g': 1.0 + 0.02 * n(ks[9], (DEPTH, D_CONF), f),
        'conf_ln_b': 0.02 * n(ks[11], (DEPTH, D_CONF), f),
        'ssm_conv_w': n(ks[12], (DEPTH, SSM_CONV, XBC_DIM), f) * SSM_CONV ** -0.5,
        'ssm_conv_b': 0.02 * n(ks[13], (DEPTH, XBC_DIM), f),
        'ssm_dt_bias': dt0 + jnp.log(-jnp.expm1(-dt0)),
        'ssm_a_log': jnp.log(jax.random.uniform(ks[14], (DEPTH, SSM_HEADS), f, 1.0, 16.0)),
        'ssm_d': 1.0 + 0.1 * n(ks[15], (DEPTH, SSM_HEADS), f),
        'ssm_norm_g': 1.0 + 0.02 * n(ks[16], (DEPTH, D_SSM), f),
        'w_out': n(ks[17], (DEPTH, D_MIX, D_MODEL), f) * D_MIX ** -0.5,
        'ln_ffn': 1.0 + 0.02 * n(ks[18], (DEPTH, D_MODEL), f),
        'router_group_w': n(ks[19], (DEPTH, D_MODEL, MOE_GROUPS), f) * D_MODEL ** -0.5,
        'router_group_b': 0.01 * n(ks[20], (DEPTH, MOE_GROUPS), f),
        'router_expert_w': n(ks[21], (DEPTH, MOE_GROUPS, D_MODEL, EXPERTS_PER_GROUP), f) * D_MODEL ** -0.5,
        'router_expert_b': 0.01 * n(ks[22], (DEPTH, MOE_GROUPS, EXPERTS_PER_GROUP), f),
        'expert_w_gate': n(ks[23], (DEPTH, N_EXPERTS, D_MODEL, D_EXPERT), f) * D_MODEL ** -0.5,
        'expert_w_up': n(ks[24], (DEPTH, N_EXPERTS, D_MODEL, D_EXPERT), f) * D_MODEL ** -0.5,
        'expert_w_down': n(ks[25], (DEPTH, N_EXPERTS, D_EXPERT, D_MODEL), f) * D_EXPERT ** -0.5,
        'ln_final': 1.0 + 0.02 * n(ks[26], (D_MODEL,), f),
    }


def reference(x_prompt, x_sample, state_conf_conv, state_ssm_conv, state_ssm,
              ln_mix, w_in, conf_dw_w, conf_dw_b, conf_ln_g, conf_ln_b,
              ssm_conv_w, ssm_conv_b, ssm_dt_bias, ssm_a_log, ssm_d, ssm_norm_g, w_out,
              ln_ffn, router_group_w, router_group_b, router_expert_w, router_expert_b,
              expert_w_gate, expert_w_up, expert_w_down, ln_final):
    bp = x_prompt.shape[0]
    xp, xs = x_prompt, x_sample
    pc, ps, ph, sc, ss, sh = [], [], [], [], [], []
    for l in range(DEPTH):
        lw = (w_in[l], conf_dw_w[l], conf_dw_b[l], conf_ln_g[l], conf_ln_b[l],
              ssm_conv_w[l], ssm_conv_b[l], ssm_dt_bias[l], ssm_a_log[l], ssm_d[l], ssm_norm_g[l], w_out[l])
        mw = (router_group_w[l], router_group_b[l], router_expert_w[l], router_expert_b[l],
              expert_w_gate[l], expert_w_up[l], expert_w_down[l])
        zc = jnp.zeros((bp, CONF_KERNEL - 1, D_CONF), xp.dtype)
        zs = jnp.zeros((bp, SSM_CONV - 1, XBC_DIM), xp.dtype)
        zh = jnp.zeros((bp, SSM_HEADS, SSM_HEAD_DIM, SSM_STATE), state_ssm.dtype)
        mp, c1, c2, c3 = mixer(rmsnorm(xp, ln_mix[l]), zc, zs, zh, *lw)
        xp = xp + mp
        xp = xp + hier_moe(rmsnorm(xp, ln_ffn[l]), *mw)
        msm, d1, d2, d3 = mixer(rmsnorm(xs, ln_mix[l]), state_conf_conv[l], state_ssm_conv[l], state_ssm[l], *lw)
        xs = xs + msm
        xs = xs + hier_moe(rmsnorm(xs, ln_ffn[l]), *mw)
        pc.append(c1); ps.append(c2); ph.append(c3)
        sc.append(d1); ss.append(d2); sh.append(d3)
    y_prompt = rmsnorm(xp, ln_final)
    y_sample = rmsnorm(xs, ln_final)
    return (y_prompt, y_sample, jnp.stack(pc), jnp.stack(ps), jnp.stack(ph),
            jnp.stack(sc), jnp.stack(ss), jnp.stack(sh))
```

```python
import functools

import jax
import jax.numpy as jnp
from jax import lax
from jax.experimental import pallas as pl
from jax.experimental.pallas import tpu as pltpu

F32 = jnp.float32
BF16 = jnp.bfloat16
I32 = jnp.int32
EPS = 1e-6

LANES = 128
SUBLANES = 8
VMEM_LIMIT_BYTES = 48 * 1024 * 1024

SSD_CHUNK = 128
CONF_HALO = 32
SSM_HALO = 8
ROW_TILE = 256
EXPERT_TILE = 256


def _cparams(*sem):
    return pltpu.CompilerParams(dimension_semantics=sem, vmem_limit_bytes=VMEM_LIMIT_BYTES)


def _dot(a, b):
    return jnp.dot(a, b, preferred_element_type=F32)


def _split3(x):
    hi = x.astype(BF16)
    r1 = x - hi.astype(F32)
    mid = r1.astype(BF16)
    lo = (r1 - mid.astype(F32)).astype(BF16)
    return hi, mid, lo


def _dot_x3(x, m_bf16):
    hi, mid, lo = _split3(x)
    return _dot(hi, m_bf16) + _dot(mid, m_bf16) + _dot(lo, m_bf16)


def _dot_m3(m_bf16, x):
    hi, mid, lo = _split3(x)
    return _dot(m_bf16, hi) + _dot(m_bf16, mid) + _dot(m_bf16, lo)


def _dot_f32(x, w):
    xh, xm, xl = _split3(x)
    wh, wm, wl = _split3(w)
    small = _dot(xh, wl) + _dot(xl, wh) + _dot(xm, wm)
    return (_dot(xh, wh) + (_dot(xh, wm) + _dot(xm, wh))) + small


def _silu(x):
    return x * jax.nn.sigmoid(x)


def _inproj_body(x_ref, g_ref, wa_ref, wb_ref, wz_ref, wx_ref, wd_ref,
                 u_ref, z_ref, xbc_ref, dt_ref):
    x = x_ref[...]
    h = x * lax.rsqrt(jnp.mean(x * x, axis=-1, keepdims=True) + EPS) * g_ref[...]
    hb = h.astype(BF16)
    a = _dot(hb, wa_ref[...])
    b = _dot(hb, wb_ref[...])
    u_ref[...] = a * jax.nn.sigmoid(b)
    z_ref[...] = _dot(hb, wz_ref[...])
    xbc_ref[...] = _dot(hb, wx_ref[...])
    dt_ref[...] = _dot(hb, wd_ref[...])


def _inproj(x2d, g, wa, wb, wz, wx, wd):
    t, d = x2d.shape
    dc, ds, xb = wa.shape[1], wz.shape[1], wx.shape[1]
    tm = min(ROW_TILE, t)
    row = lambda i: (i, 0)
    fixed = lambda i: (0, 0)
    return pl.pallas_call(
        _inproj_body,
        grid=(t // tm,),
        in_specs=[pl.BlockSpec((tm, d), row), pl.BlockSpec((1, d), fixed),
                  pl.BlockSpec((d, dc), fixed), pl.BlockSpec((d, dc), fixed),
                  pl.BlockSpec((d, ds), fixed), pl.BlockSpec((d, xb), fixed),
                  pl.BlockSpec((d, LANES), fixed)],
        out_specs=[pl.BlockSpec((tm, dc), row), pl.BlockSpec((tm, ds), row),
                   pl.BlockSpec((tm, xb), row), pl.BlockSpec((tm, LANES), row)],
        out_shape=[jax.ShapeDtypeStruct((t, dc), F32), jax.ShapeDtypeStruct((t, ds), F32),
                   jax.ShapeDtypeStruct((t, xb), F32), jax.ShapeDtypeStruct((t, LANES), F32)],
        compiler_params=_cparams("arbitrary"),
        name="inproj",
    )(x2d, g, wa, wb, wz, wx, wd)


def _confconv_body(u_ref, halo_ref, w_ref, b_ref, g_ref, beta_ref, v_ref, full_ref,
                   *, tl, rc, lc, kw, nl):
    c = u_ref.shape[-1]
    j = pl.program_id(1)
    base = CONF_HALO - (kw - 1)

    @pl.when(j == 0)
    def _():
        full_ref[0:SUBLANES, :] = jnp.zeros((SUBLANES, c), F32)
        full_ref[base:CONF_HALO, :] = halo_ref[0]

    full_ref[CONF_HALO:CONF_HALO + tl, :] = u_ref[0]

    for r0 in range(0, tl, rc):
        chunks = []
        for c0 in range(0, c, lc):
            acc = jnp.broadcast_to(b_ref[:, c0:c0 + lc], (rc, lc))
            for k in range(kw):
                s = base + r0 + k
                acc = acc + w_ref[k:k + 1, c0:c0 + lc] * full_ref[s:s + rc, c0:c0 + lc]
            chunks.append(acc)
        tot = chunks[0].sum(axis=-1, keepdims=True)
        for ch in chunks[1:]:
            tot = tot + ch.sum(axis=-1, keepdims=True)
        mu = tot * (1.0 / c)
        cen = [ch - mu for ch in chunks]
        sq = (cen[0] * cen[0]).sum(axis=-1, keepdims=True)
        for ch in cen[1:]:
            sq = sq + (ch * ch).sum(axis=-1, keepdims=True)
        inv = lax.rsqrt(sq * (1.0 / c) + EPS)
        for n, ch in enumerate(cen):
            c0 = n * lc
            yv = ch * inv * g_ref[:, c0:c0 + lc] + beta_ref[:, c0:c0 + lc]
            v_ref[0, r0:r0 + rc, c0:c0 + lc] = _silu(yv).astype(v_ref.dtype)

    if nl > 1:
        full_ref[0:CONF_HALO, :] = full_ref[tl:tl + CONF_HALO, :]


def _confconv(u3, halo, w, b, g, beta):
    bsz, l, c = u3.shape
    kw = w.shape[0]
    tl = min(128, l)
    rc = min(32, tl)
    lc = min(256, c)
    nl = l // tl
    fixed = lambda i, j: (0, 0)
    body = functools.partial(_confconv_body, tl=tl, rc=rc, lc=lc, kw=kw, nl=nl)
    return pl.pallas_call(
        body,
        grid=(bsz, nl),
        in_specs=[pl.BlockSpec((1, tl, c), lambda i, j: (i, j, 0)),
                  pl.BlockSpec((1, kw - 1, c), lambda i, j: (i, 0, 0)),
                  pl.BlockSpec((kw, c), fixed), pl.BlockSpec((1, c), fixed),
                  pl.BlockSpec((1, c), fixed), pl.BlockSpec((1, c), fixed)],
        out_specs=pl.BlockSpec((1, tl, c), lambda i, j: (i, j, 0)),
        out_shape=jax.ShapeDtypeStruct((bsz, l, c), BF16),
        scratch_shapes=[pltpu.VMEM((CONF_HALO + tl, c), F32)],
        compiler_params=_cparams("arbitrary", "arbitrary"),
        name="confconv",
    )(u3, halo, w, b, g, beta)


def _ssd_body(xbc_ref, dtr_ref, z_ref, halo_ref, h0_ref, cw_ref, cb_ref, dtb_ref, alog_ref,
              dsk_ref, ng_ref, expand_ref, tril_ref,
              y_ref, hfin_ref,
              full_ref, xc_ref, ex_ref, yz_ref, st_ref,
              *, lq, q, ds, n, groups, kw, nch):
    xb = xbc_ref.shape[-1]
    c = pl.program_id(1)
    base = SSM_HALO - (kw - 1)
    gw = ds // groups
    hd = LANES // 2

    @pl.when(c == 0)
    def _():
        if lq < q:
            full_ref[...] = jnp.zeros(full_ref.shape, F32)
        else:
            full_ref[0:SUBLANES, :] = jnp.zeros((SUBLANES, xb), F32)
        full_ref[base:SSM_HALO, :] = halo_ref[0]
        for p0 in range(0, ds, LANES):
            st_ref[:, p0:p0 + LANES] = h0_ref[0, p0:p0 + LANES, :].T

    full_ref[SSM_HALO:SSM_HALO + lq, :] = xbc_ref[0]

    lc = 256
    for c0 in range(0, xb, lc):
        acc = jnp.broadcast_to(cb_ref[:, c0:c0 + lc], (q, lc))
        for k in range(kw):
            acc = acc + cw_ref[k:k + 1, c0:c0 + lc] * full_ref[base + k:base + k + q, c0:c0 + lc]
        xc_ref[:, c0:c0 + lc] = _silu(acc)
    if nch > 1:
        full_ref[0:SSM_HALO, :] = full_ref[q:q + SSM_HALO, :]

    dt = jax.nn.softplus(_pad_rows(dtr_ref[0], q) + dtb_ref[...])
    if lq < q:
        dt = jnp.where(lax.broadcasted_iota(I32, (q, LANES), 0) < lq, dt, 0.0)
    a = -jnp.exp(alog_ref[...])
    acum = _dot_m3(tril_ref[...], dt * a)
    a_last = acum[q - 1:q, :]
    ea = jnp.exp(acum)
    sdec = dt * jnp.exp(a_last - acum)
    acum_t = acum.T
    dt_t = dt.T
    ex_ref[0:q, :] = _dot_x3(sdec, expand_ref[...])
    ex_ref[q:2 * q, :] = _dot_x3(ea, expand_ref[...])

    ii = lax.broadcasted_iota(I32, (q, q), 0)
    jj = lax.broadcasted_iota(I32, (q, q), 1)
    causal = ii >= jj
    lane = lax.broadcasted_iota(I32, (q, LANES), 1)

    ssq = []
    for g in range(groups):
        b_g = xc_ref[:, ds + g * n:ds + (g + 1) * n]
        c_g = xc_ref[:, ds + groups * n + g * n:ds + groups * n + (g + 1) * n]
        c_gb = c_g.astype(BF16)
        b_gb = b_g.astype(BF16)
        cb = lax.dot_general(c_gb, b_gb, (((1,), (1,)), ((), ())), preferred_element_type=F32)
        bt_gb = b_g.T.astype(BF16)
        gsq = jnp.zeros((q, 1), F32)
        for pp in range(gw // LANES):
            p0 = g * gw + pp * LANES
            x_p = xc_ref[:, p0:p0 + LANES]
            x_pb = x_p.astype(BF16)
            halves = []
            for hh in range(2):
                h = (p0 // hd) + hh
                seg = acum[:, h:h + 1] - acum_t[h:h + 1, :]
                dec = jnp.exp(jnp.where(causal, seg, -jnp.inf))
                m = (cb * dec * dt_t[h:h + 1, :]).astype(BF16)
                halves.append(_dot(m, x_pb))
            y_d = jnp.where(lane < hd, halves[0], halves[1])
            st_old = st_ref[:, p0:p0 + LANES]
            y_off = _dot(c_gb, st_old.astype(BF16)) * ex_ref[q:2 * q, p0:p0 + LANES]
            y_p = y_d + y_off + dsk_ref[:, p0:p0 + LANES] * x_p
            yz = y_p * _silu(_pad_rows(z_ref[0, :, p0:p0 + LANES], q))
            yz_ref[:, p0:p0 + LANES] = yz
            gsq = gsq + (yz * yz).sum(axis=-1, keepdims=True)
            xs_b = (x_p * ex_ref[0:q, p0:p0 + LANES]).astype(BF16)
            cd = ex_ref[2 * q - 1:2 * q, p0:p0 + LANES]
            st_ref[:, p0:p0 + LANES] = cd * st_old + _dot(bt_gb, xs_b)
        ssq.append(gsq)

    for g in range(groups):
        inv = lax.rsqrt(ssq[g] * (1.0 / gw) + EPS)
        for pp in range(gw // LANES):
            p0 = g * gw + pp * LANES
            yn = yz_ref[:, p0:p0 + LANES] * inv * ng_ref[:, p0:p0 + LANES]
            y_ref[0, :, p0:p0 + LANES] = yn[0:lq, :].astype(y_ref.dtype)

    @pl.when(c == nch - 1)
    def _():
        for p0 in range(0, ds, LANES):
            hfin_ref[0, p0:p0 + LANES, :] = st_ref[:, p0:p0 + LANES].T


def _pad_rows(v, q):
    if v.shape[0] == q:
        return v
    return jnp.concatenate([v, jnp.zeros((q - v.shape[0], v.shape[1]), v.dtype)], axis=0)


def _ssd(xbc3, dtr3, z3, halo, h0, cw, cb, dtb, alog, dsk, ng, expand, tril, n, groups):
    bsz, l, xb = xbc3.shape
    ds = z3.shape[-1]
    kw = cw.shape[0]
    q = SSD_CHUNK
    lq = min(q, l)
    nch = l // lq
    fixed = lambda i, j: (0, 0)
    blk = lambda i, j: (i, j, 0)
    per_b = lambda i, j: (i, 0, 0)
    body = functools.partial(_ssd_body, lq=lq, q=q, ds=ds, n=n, groups=groups, kw=kw, nch=nch)
    return pl.pallas_call(
        body,
        grid=(bsz, nch),
        in_specs=[pl.BlockSpec((1, lq, xb), blk), pl.BlockSpec((1, lq, LANES), blk),
                  pl.BlockSpec((1, lq, ds), blk),
                  pl.BlockSpec((1, kw - 1, xb), per_b), pl.BlockSpec((1, ds, n), per_b),
                  pl.BlockSpec((kw, xb), fixed), pl.BlockSpec((1, xb), fixed),
                  pl.BlockSpec((1, LANES), fixed), pl.BlockSpec((1, LANES), fixed),
                  pl.BlockSpec((1, ds), fixed), pl.BlockSpec((1, ds), fixed),
                  pl.BlockSpec((LANES, ds), fixed), pl.BlockSpec((q, q), fixed)],
        out_specs=[pl.BlockSpec((1, lq, ds), blk), pl.BlockSpec((1, ds, n), per_b)],
        out_shape=[jax.ShapeDtypeStruct((bsz, l, ds), BF16),
                   jax.ShapeDtypeStruct((bsz, ds, n), F32)],
        scratch_shapes=[pltpu.VMEM((SSM_HALO + q, xb), F32), pltpu.VMEM((q, xb), F32),
                        pltpu.VMEM((2 * q, ds), F32), pltpu.VMEM((q, ds), F32),
                        pltpu.VMEM((n, ds), F32)],
        compiler_params=_cparams("arbitrary", "arbitrary"),
        name="ssd",
    )(xbc3, dtr3, z3, halo, h0, cw, cb, dtb, alog, dsk, ng, expand, tril)


def _outproj_body(v_ref, y_ref, xr_ref, wt_ref, wb_ref, g_ref, rw_ref, rb_ref, cin_ref, stril_ref,
                  xres_ref, xn_ref, ri_ref, rf_ref, cout_ref, carry_ref, *, n_groups, epg):
    i = pl.program_id(0)
    tm = v_ref.shape[0]

    @pl.when(i == 0)
    def _():
        carry_ref[...] = cin_ref[...]

    xr = xr_ref[...] + (_dot(v_ref[...], wt_ref[...]) + _dot(y_ref[...], wb_ref[...]))
    xres_ref[...] = xr
    xn = xr * lax.rsqrt(jnp.mean(xr * xr, axis=-1, keepdims=True) + EPS) * g_ref[...]
    xn_ref[...] = xn

    logits = _dot_f32(xn, rw_ref[...]) + rb_ref[...]
    lanei = lax.broadcasted_iota(I32, (tm, LANES), 1)
    lane = lanei.astype(F32)
    neg = -jnp.inf
    is_g = lanei < n_groups
    gl = jnp.where(is_g, logits, neg)
    gmax = gl.max(axis=-1, keepdims=True)
    grp = jnp.where(gl == gmax, lane, float(LANES)).min(axis=-1, keepdims=True)
    g1 = 1.0 / jnp.where(is_g, jnp.exp(logits - gmax), 0.0).sum(axis=-1, keepdims=True)
    lo = n_groups + grp * epg
    el = jnp.where((lane >= lo) & (lane < lo + epg), logits, neg)
    m1 = el.max(axis=-1, keepdims=True)
    i1 = jnp.where(el == m1, lane, float(LANES)).min(axis=-1, keepdims=True)
    el2 = jnp.where(lane == i1, neg, el)
    m2 = el2.max(axis=-1, keepdims=True)
    i2 = jnp.where(el2 == m2, lane, float(LANES)).min(axis=-1, keepdims=True)
    t = jnp.exp(m2 - m1)
    w0 = g1 * (1.0 / (1.0 + t))
    w1 = g1 * (t / (1.0 + t))
    e0 = i1 - n_groups
    e1 = i2 - n_groups

    oh0 = lane == e0
    oh1 = lane == e1
    both = jnp.where(oh0, 1.0, 0.0) + jnp.where(oh1, 1.0, 0.0)
    tot = carry_ref[...] + _dot(stril_ref[...], both.astype(BF16))
    r0 = jnp.where(oh0, tot, 0.0).sum(axis=-1, keepdims=True)
    r1 = jnp.where(oh1, tot, 0.0).sum(axis=-1, keepdims=True)
    carry_ref[...] = carry_ref[...] + both.sum(axis=0, keepdims=True)

    packed = jnp.where(lanei == 0, e0, jnp.where(lanei == 1, e1,
                       jnp.where(lanei == 2, r0, jnp.where(lanei == 3, r1, 0.0))))
    ri_ref[...] = packed.astype(I32)
    rf_ref[...] = jnp.where(lanei == 0, w0, jnp.where(lanei == 1, w1, 0.0))
    cout_ref[...] = carry_ref[...]


def _outproj_route(v2, y2, xr2, wt, wb, g, rw, rb, cin, n_groups, epg):
    t, d = xr2.shape
    dc, ds = v2.shape[1], y2.shape[1]
    tm = min(ROW_TILE, t)
    stril = (jnp.arange(tm)[:, None] > jnp.arange(tm)[None, :]).astype(BF16)
    row = lambda i: (i, 0)
    fixed = lambda i: (0, 0)
    body = functools.partial(_outproj_body, n_groups=n_groups, epg=epg)
    return pl.pallas_call(
        body,
        grid=(t // tm,),
        in_specs=[pl.BlockSpec((tm, dc), row), pl.BlockSpec((tm, ds), row), pl.BlockSpec((tm, d), row),
                  pl.BlockSpec((dc, d), fixed), pl.BlockSpec((ds, d), fixed),
                  pl.BlockSpec((1, d), fixed), pl.BlockSpec((d, LANES), fixed),
                  pl.BlockSpec((1, LANES), fixed), pl.BlockSpec((1, LANES), fixed),
                  pl.BlockSpec((tm, tm), fixed)],
        out_specs=[pl.BlockSpec((tm, d), row), pl.BlockSpec((tm, d), row),
                   pl.BlockSpec((tm, LANES), row), pl.BlockSpec((tm, LANES), row),
                   pl.BlockSpec((1, LANES), fixed)],
        out_shape=[jax.ShapeDtypeStruct((t, d), F32), jax.ShapeDtypeStruct((t, d), F32),
                   jax.ShapeDtypeStruct((t, LANES), I32), jax.ShapeDtypeStruct((t, LANES), F32),
                   jax.ShapeDtypeStruct((1, LANES), F32)],
        scratch_shapes=[pltpu.VMEM((1, LANES), F32)],
        compiler_params=_cparams("arbitrary"),
        name="outproj_route",
    )(v2, y2, xr2, wt, wb, g, rw, rb, cin, stril)


def _scatter_body(pos_ref, xa_ref, xb_ref, xs_ref, sem, *, tm, na):
    i = pl.program_id(0)

    def run(src_ref):
        def copy(r, k):
            dst = pos_ref[(i * tm + r) * 2 + k]
            return pltpu.make_async_copy(src_ref.at[pl.ds(r, 1)], xs_ref.at[pl.ds(dst, 1)], sem)

        def issue(r, carry):
            copy(r, 0).start()
            copy(r, 1).start()
            return carry

        def drain(r, carry):
            copy(r, 0).wait()
            copy(r, 1).wait()
            return carry

        lax.fori_loop(0, tm, issue, 0)
        lax.fori_loop(0, tm, drain, 0)

    @pl.when(i < na)
    def _():
        run(xa_ref)

    @pl.when(i >= na)
    def _():
        run(xb_ref)


def _scatter_rows(pos_flat, xa, xb):
    (ta, d), tb = xa.shape, xb.shape[0]
    tm = min(ROW_TILE, ta, tb)
    na, nb = ta // tm, tb // tm
    grid_spec = pltpu.PrefetchScalarGridSpec(
        num_scalar_prefetch=1,
        grid=(na + nb,),
        in_specs=[pl.BlockSpec((tm, d), lambda i, pos: (jnp.minimum(i, na - 1), 0)),
                  pl.BlockSpec((tm, d), lambda i, pos: (jnp.maximum(i - na, 0), 0))],
        out_specs=pl.BlockSpec(memory_space=pl.ANY),
        scratch_shapes=[pltpu.SemaphoreType.DMA(())],
    )
    return pl.pallas_call(
        functools.partial(_scatter_body, tm=tm, na=na),
        grid_spec=grid_spec,
        out_shape=jax.ShapeDtypeStruct((2 * (ta + tb), d), F32),
        compiler_params=_cparams("arbitrary"),
        name="scatter_rows",
    )(pos_flat, xa, xb)


def _experts_body(tw_ref, ew_ref, lo_ref, hi_ref, nw_ref, xs_ref, wg_ref, wu_ref, wd_ref, ys_ref,
                  wgb_ref, wub_ref, wdb_ref):
    w = pl.program_id(0)
    tr = xs_ref.shape[0]
    prev = jnp.maximum(w - 1, 0)

    @pl.when((w == 0) | (ew_ref[w] != ew_ref[prev]))
    def _():
        wgb_ref[...] = wg_ref[0].astype(BF16)
        wub_ref[...] = wu_ref[0].astype(BF16)
        wdb_ref[...] = wd_ref[0].astype(BF16)

    @pl.when(w < nw_ref[0])
    def _():
        x = xs_ref[...].astype(BF16)
        hid = _silu(_dot(x, wgb_ref[...])) * _dot(x, wub_ref[...])
        y = _dot(hid.astype(BF16), wdb_ref[...])
        row = tw_ref[w] * tr + lax.broadcasted_iota(I32, (tr, 1), 0)
        mine = (row >= lo_ref[w]) & (row < hi_ref[w])
        first = (w == 0) | (tw_ref[w] != tw_ref[prev])

        @pl.when(first)
        def _():
            ys_ref[...] = jnp.where(mine, y, 0.0)

        @pl.when(jnp.logical_not(first))
        def _():
            ys_ref[...] = jnp.where(mine, y, ys_ref[...])


def _experts(tw, ew, lo, hi, nw, xs, wg, wu, wd):
    p, d = xs.shape
    de = wg.shape[-1]
    tr = EXPERT_TILE
    rows = lambda w, tw, ew, lo, hi, nw: (tw[w], 0)
    wsel = lambda w, tw, ew, lo, hi, nw: (ew[w], 0, 0)
    grid_spec = pltpu.PrefetchScalarGridSpec(
        num_scalar_prefetch=5,
        grid=(tw.shape[0],),
        in_specs=[pl.BlockSpec((tr, d), rows), pl.BlockSpec((1, d, de), wsel),
                  pl.BlockSpec((1, d, de), wsel), pl.BlockSpec((1, de, d), wsel)],
        out_specs=pl.BlockSpec((tr, d), rows),
        scratch_shapes=[pltpu.VMEM((d, de), BF16), pltpu.VMEM((d, de), BF16),
                        pltpu.VMEM((de, d), BF16)],
    )
    return pl.pallas_call(
        _experts_body,
        grid_spec=grid_spec,
        out_shape=jax.ShapeDtypeStruct((p, d), F32),
        compiler_params=_cparams("arbitrary"),
        name="experts",
    )(tw, ew, lo, hi, nw, xs, wg, wu, wd)


def _combine_body(pos_ref, xres_ref, rf_ref, g_ref, ys_ref, out_ref, buf_ref, sem, *, tm, blk0):
    i = pl.program_id(0)

    def copy(r, k):
        src = pos_ref[((i + blk0) * tm + r) * 2 + k]
        return pltpu.make_async_copy(ys_ref.at[pl.ds(src, 1)], buf_ref.at[k, pl.ds(r, 1)], sem)

    def issue(r, carry):
        copy(r, 0).start()
        copy(r, 1).start()
        return carry

    def drain(r, carry):
        copy(r, 0).wait()
        copy(r, 1).wait()
        return carry

    lax.fori_loop(0, tm, issue, 0)
    lax.fori_loop(0, tm, drain, 0)

    rf = rf_ref[...]
    w0 = rf[:, 0:1]
    w1 = rf[:, 1:2]
    x = xres_ref[...] + (w0 * buf_ref[0] + w1 * buf_ref[1])
    out_ref[...] = x * lax.rsqrt(jnp.mean(x * x, axis=-1, keepdims=True) + EPS) * g_ref[...]


def _combine(pos_flat, xres, rf, g, ys, row0):
    t, d = xres.shape
    tm = min(ROW_TILE, t)
    blk0 = row0 // tm
    grid_spec = pltpu.PrefetchScalarGridSpec(
        num_scalar_prefetch=1,
        grid=(t // tm,),
        in_specs=[pl.BlockSpec((tm, d), lambda i, pos: (i, 0)),
                  pl.BlockSpec((tm, LANES), lambda i, pos: (i, 0)),
                  pl.BlockSpec((1, d), lambda i, pos: (0, 0)),
                  pl.BlockSpec(memory_space=pl.ANY)],
        out_specs=pl.BlockSpec((tm, d), lambda i, pos: (i, 0)),
        scratch_shapes=[pltpu.VMEM((2, tm, d), F32), pltpu.SemaphoreType.DMA(())],
    )
    return pl.pallas_call(
        functools.partial(_combine_body, tm=tm, blk0=blk0),
        grid_spec=grid_spec,
        out_shape=jax.ShapeDtypeStruct((t, d), F32),
        compiler_params=_cparams("arbitrary"),
        name="combine",
    )(pos_flat, xres, rf, g, ys)


def _pad_lanes(v):
    return jnp.pad(v.astype(F32), (0, LANES - v.shape[0]))[None, :]


def kernel(x_prompt, x_sample, state_conf_conv, state_ssm_conv, state_ssm, ln_mix, w_in, conf_dw_w,
           conf_dw_b, conf_ln_g, conf_ln_b, ssm_conv_w, ssm_conv_b, ssm_dt_bias, ssm_a_log, ssm_d,
           ssm_norm_g, w_out, ln_ffn, router_group_w, router_group_b, router_expert_w,
           router_expert_b, expert_w_gate, expert_w_up, expert_w_down, ln_final):
    depth = w_in.shape[0]
    assert depth == 1, "single-layer step"
    bp, lp, d = x_prompt.shape
    bs, ls, _ = x_sample.shape
    dc = conf_dw_w.shape[-1]
    xb = ssm_conv_w.shape[-1]
    ds = ssm_norm_g.shape[-1]
    heads = ssm_dt_bias.shape[-1]
    hd = ds // heads
    n = state_ssm.shape[-1]
    groups = (xb - ds) // (2 * n)
    n_groups = router_group_w.shape[-1]
    epg = router_expert_w.shape[-1]
    n_exp = n_groups * epg
    assert hd * 2 == LANES and heads <= LANES and n_groups + n_exp <= LANES
    tp, ts = bp * lp, bs * ls
    t_all = tp + ts

    wi = w_in[0]
    wa = wi[:, :dc].astype(BF16)
    wb = wi[:, dc:2 * dc].astype(BF16)
    wz = wi[:, 2 * dc:2 * dc + ds].astype(BF16)
    wx = wi[:, 2 * dc + ds:2 * dc + ds + xb].astype(BF16)
    wd = jnp.pad(wi[:, 2 * dc + ds + xb:], ((0, 0), (0, LANES - heads))).astype(BF16)
    g_mix = ln_mix[0][None, :]
    dtb = _pad_lanes(ssm_dt_bias[0])
    alog = _pad_lanes(ssm_a_log[0])
    dsk = jnp.repeat(ssm_d[0].astype(F32), hd)[None, :]
    ng = ssm_norm_g[0][None, :]
    expand = (jnp.arange(ds)[None, :] // hd == jnp.arange(LANES)[:, None]).astype(BF16)
    q = SSD_CHUNK
    tril = (jnp.arange(q)[:, None] >= jnp.arange(q)[None, :]).astype(BF16)
    wt = w_out[0][:dc].astype(BF16)
    wbot = w_out[0][dc:].astype(BF16)
    g_ffn = ln_ffn[0][None, :]
    rw = jnp.concatenate([router_group_w[0],
                          jnp.transpose(router_expert_w[0], (1, 0, 2)).reshape(d, n_exp)], axis=1)
    rw = jnp.pad(rw, ((0, 0), (0, LANES - n_groups - n_exp))).astype(F32)
    rb = jnp.concatenate([router_group_b[0], router_expert_b[0].reshape(n_exp)])
    rb = jnp.pad(rb, (0, LANES - n_groups - n_exp))[None, :].astype(F32)

    def mix(x3, conf_halo, ssm_halo, h0):
        b, l, _ = x3.shape
        u, z, xbc, dtr = _inproj(x3.reshape(b * l, d), g_mix, wa, wb, wz, wx, wd)
        u3 = u.reshape(b, l, dc)
        xbc3 = xbc.reshape(b, l, xb)
        v = _confconv(u3, conf_halo, conf_dw_w[0], conf_dw_b[0][None, :],
                      conf_ln_g[0][None, :], conf_ln_b[0][None, :])
        y, hfin = _ssd(xbc3, dtr.reshape(b, l, LANES), z.reshape(b, l, ds), ssm_halo,
                       h0.reshape(b, ds, n), ssm_conv_w[0], ssm_conv_b[0][None, :],
                       dtb, alog, dsk, ng, expand, tril, n, groups)
        return u3, xbc3, v.reshape(b * l, dc), y.reshape(b * l, ds), hfin.reshape(b, heads, hd, n)

    kc, ks = conf_dw_w.shape[1], ssm_conv_w.shape[1]
    zc = jnp.zeros((bp, kc - 1, dc), F32)
    zs = jnp.zeros((bp, ks - 1, xb), F32)
    zh = jnp.zeros((bp, heads, hd, n), F32)
    up, xbcp, vp, yp, hp = mix(x_prompt, zc, zs, zh)
    us, xbcs, vs, ys_, hs = mix(x_sample, state_conf_conv[0], state_ssm_conv[0], state_ssm[0])

    counts0 = jnp.zeros((1, LANES), F32)
    xres_p, xn_p, rip, rfp, counts1 = _outproj_route(
        vp, yp, x_prompt.reshape(tp, d), wt, wbot, g_ffn, rw, rb, counts0, n_groups, epg)
    xres_s, xn_s, ris, rfs, counts2 = _outproj_route(
        vs, ys_, x_sample.reshape(ts, d), wt, wbot, g_ffn, rw, rb, counts1, n_groups, epg)

    tr = EXPERT_TILE
    assert (2 * t_all) % tr == 0
    counts = counts2[0, :n_exp].astype(I32)
    ends = jnp.cumsum(counts)
    offs = ends - counts
    ri = jnp.concatenate([rip, ris], axis=0)
    pos = jnp.take(offs, ri[:, 0:2], axis=0) + ri[:, 2:4]
    pos_flat = pos.reshape(-1).astype(I32)
    first_tile = offs // tr
    n_items = jnp.where(counts > 0, (ends - 1) // tr - first_tile + 1, 0)
    cum_items = jnp.cumsum(n_items)
    total = cum_items[-1]
    n_work = (2 * t_all) // tr + n_exp - 1
    wv = jnp.minimum(jnp.arange(n_work), total - 1)
    ew = jnp.sum(cum_items[None, :] <= wv[:, None], axis=1).astype(I32)
    tw = (jnp.take(first_tile, ew) + (wv - jnp.take(cum_items - n_items, ew))).astype(I32)
    lo = jnp.take(offs, ew).astype(I32)
    hi = jnp.take(ends, ew).astype(I32)
    nw = total[None].astype(I32)

    xs = _scatter_rows(pos_flat, xn_p, xn_s)
    ysorted = _experts(tw, ew, lo, hi, nw, xs, expert_w_gate[0], expert_w_up[0], expert_w_down[0])
    g_fin = ln_final[None, :]
    y_prompt = _combine(pos_flat, xres_p, rfp, g_fin, ysorted, 0).reshape(bp, lp, d)
    y_sample = _combine(pos_flat, xres_s, rfs, g_fin, ysorted, tp).reshape(bs, ls, d)

    new_conf_p = up[:, lp - (kc - 1):, :][None]
    new_ssmc_p = xbcp[:, lp - (ks - 1):, :][None]
    conf_full_s = jnp.concatenate([state_conf_conv[0], us], axis=1)
    new_conf_s = conf_full_s[:, conf_full_s.shape[1] - (kc - 1):, :][None]
    ssm_full_s = jnp.concatenate([state_ssm_conv[0], xbcs], axis=1)
    new_ssmc_s = ssm_full_s[:, ssm_full_s.shape[1] - (ks - 1):, :][None]
    return (y_prompt, y_sample, new_conf_p, new_ssmc_p, hp[None],
            new_conf_s, new_ssmc_s, hs[None])
```
